```python
import math
import jax, jax.numpy as jnp
from jax import lax
import numpy as np

D_MODEL = 2048
BATCH = 4
SEQ = 2048
DEPTH = 2
DEC_BATCH = 16
DEC_SEQ = 16
PAST_LEN = 4096

CHUNK = 64
Q_BLOCK = 128
N_AB = (DEPTH + 1) // 2
N_SB = DEPTH // 2
SSD_HEAD_DIM = 64
SSD_D = D_MODEL
SSD_HEADS = SSD_D // SSD_HEAD_DIM
SSD_GROUPS = 4
SSD_STATE = 128
CONV_W = 4
CONV_DIM = SSD_D + 2 * SSD_GROUPS * SSD_STATE
FOX_HEAD_DIM = 128
FOX_D = D_MODEL
FOX_HEADS = FOX_D // FOX_HEAD_DIM
SB_HEAD_DIM = 128
SB_D = D_MODEL
SB_HEADS = SB_D // SB_HEAD_DIM
MEM_LEN = 256
CA_HEADS = 4
CA_HEAD_DIM = D_MODEL // CA_HEADS
CA_D = CA_HEADS * CA_HEAD_DIM
D_FF = ((8 * D_MODEL // 3 + 255) // 256) * 256
FFN_HALF = 0.5
AB_IN = SSD_D + CONV_DIM + SSD_HEADS + 3 * FOX_D + FOX_HEADS
AB_MIX = SSD_D + FOX_D
RMS_EPS = 1e-6

kernel_name = "hybrid_ssd_fox_stickbreak_streaming_step"


def rms_norm(x, g):
    xf = x.astype(jnp.float32)
    y = xf * lax.rsqrt(jnp.mean(xf * xf, axis=-1, keepdims=True) + RMS_EPS)
    return (y * g.astype(jnp.float32)).astype(x.dtype)


def swiglu(x, wg, wu, wd):
    return (jax.nn.silu(x @ wg) * (x @ wu)) @ wd


def split_sizes(x, sizes):
    idx, acc = [], 0
    for s in sizes[:-1]:
        acc += s
        idx.append(acc)
    return jnp.split(x, idx, axis=-1)


def causal_conv(u, ctx, w, b):
    L = u.shape[1]
    up = jnp.concatenate([ctx.astype(u.dtype), u], axis=1)
    out = sum(up[:, k:k + L] * w[k] for k in range(CONV_W)) + b
    return out, up[:, -(CONV_W - 1):]


def ssd_scan(x, dt, a_log, bm, cm, d_skip, s0):
    bsz, L = x.shape[:2]
    q = min(CHUNK, L)
    nc = L // q
    r = SSD_HEADS // SSD_GROUPS
    f32 = jnp.float32
    A = -jnp.exp(a_log.astype(f32)).reshape(SSD_GROUPS, r)
    xs = x.reshape(bsz, nc, q, SSD_GROUPS, r, SSD_HEAD_DIM)
    dts = dt.reshape(bsz, nc, q, SSD_GROUPS, r)
    Bs = bm.reshape(bsz, nc, q, SSD_GROUPS, SSD_STATE)
    Cs = cm.reshape(bsz, nc, q, SSD_GROUPS, SSD_STATE)
    acum = jnp.cumsum(dts * A, axis=2)
    seg = acum[:, :, :, None] - acum[:, :, None, :]
    causal = jnp.tril(jnp.ones((q, q), dtype=bool))[:, :, None, None]
    decay = jnp.exp(jnp.where(causal, seg, -jnp.inf))
    cb = jnp.einsum('bclgn,bcsgn->bclsg', Cs, Bs)
    w = cb[..., None] * decay * dts[:, :, None]
    y_diag = jnp.einsum('bclsgr,bcsgrp->bclgrp', w, xs)
    decay_end = jnp.exp(acum[:, :, -1:] - acum)
    states = jnp.einsum('bcsgn,bcsgr,bcsgrp->bcgrpn', Bs, decay_end * dts, xs)
    chunk_decay = jnp.exp(acum[:, :, -1])

    def step(s, inp):
        dec, st = inp
        return dec[..., None, None] * s + st, s

    s_init = s0.astype(f32).reshape(bsz, SSD_GROUPS, r, SSD_HEAD_DIM, SSD_STATE)
    s_fin, s_in = lax.scan(step, s_init, (chunk_decay.swapaxes(0, 1), states.astype(f32).swapaxes(0, 1)))
    s_in = s_in.swapaxes(0, 1)
    y_off = jnp.einsum('bclgn,bcgrpn,bclgr->bclgrp', Cs, s_in, jnp.exp(acum))
    y = y_diag + y_off + d_skip.reshape(SSD_GROUPS, r, 1) * xs
    return (y.reshape(bsz, L, SSD_D).astype(x.dtype),
            s_fin.reshape(bsz, SSD_HEADS, SSD_HEAD_DIM, SSD_STATE).astype(s0.dtype))


def query_blocks(q, fn):
    bsz, Lq, H, dh = q.shape
    blk = min(Q_BLOCK, Lq)
    nb = Lq // blk
    qb = q.reshape(bsz, nb, blk, H, dh).swapaxes(0, 1)
    starts = jnp.arange(nb) * blk
    out = lax.map(lambda a: fn(a[0], a[1]), (qb, starts))
    return out.swapaxes(0, 1).reshape(bsz, Lq, H, out.shape[-1])


def fox_attention(q, k_all, v_all, logf_all):
    Lq, Lk = q.shape[1], k_all.shape[1]
    offset = Lk - Lq
    cT = jnp.cumsum(logf_all.astype(jnp.float32), axis=1).transpose(0, 2, 1)
    kpos = jnp.arange(Lk)
    scale = FOX_HEAD_DIM ** -0.5

    def block(q_blk, start):
        blk = q_blk.shape[1]
        qpos = offset + start + jnp.arange(blk)
        s = jnp.einsum('bqhd,bkhd->bhqk', q_blk, k_all).astype(jnp.float32) * scale
        c_q = lax.dynamic_slice_in_dim(cT, offset + start, blk, axis=2)
        s = s + c_q[..., None] - cT[:, :, None, :]
        s = jnp.where(kpos[None, :] <= qpos[:, None], s, -jnp.inf)
        p = jax.nn.softmax(s, axis=-1)
        return jnp.einsum('bhqk,bkhd->bqhd', p.astype(v_all.dtype), v_all)

    return query_blocks(q, block)


def stick_breaking_attention(q, k_all, v_all):
    Lq, Lk = q.shape[1], k_all.shape[1]
    offset = Lk - Lq
    kpos = jnp.arange(Lk)
    scale = SB_HEAD_DIM ** -0.5

    def block(q_blk, start):
        blk = q_blk.shape[1]
        qpos = offset + start + jnp.arange(blk)
        z = jnp.einsum('bqhd,bkhd->bhqk', q_blk, k_all).astype(jnp.float32) * scale
        strict = kpos[None, :] < qpos[:, None]
        log_keep = jnp.where(strict, jax.nn.log_sigmoid(-z), 0.0)
        after = lax.cumsum(log_keep, axis=3, reverse=True) - log_keep
        a = jnp.where(strict, jnp.exp(jax.nn.log_sigmoid(z) + after), 0.0)
        return jnp.einsum('bhqk,bkhd->bqhd', a.astype(v_all.dtype), v_all)

    return query_blocks(q, block)


def mixer_ab(h, conv_ctx, ssm0, k_past, v_past, logf_past, w_in, conv_w, conv_b,
             dt_bias, a_log, d_skip, norm_g, f_bias, w_out):
    bsz, L, _ = h.shape
    z, xbc, dt_raw, q, k, v, f_raw = split_sizes(
        h @ w_in, (SSD_D, CONV_DIM, SSD_HEADS, FOX_D, FOX_D, FOX_D, FOX_HEADS))
    xbc, conv_new = causal_conv(xbc, conv_ctx, conv_w, conv_b)
    xbc = jax.nn.silu(xbc)
    xs, bm, cm = split_sizes(xbc, (SSD_D, SSD_GROUPS * SSD_STATE, SSD_GROUPS * SSD_STATE))
    dt = jax.nn.softplus(dt_raw.astype(jnp.float32) + dt_bias.astype(jnp.float32))
    y, ssm_new = ssd_scan(xs.reshape(bsz, L, SSD_HEADS, SSD_HEAD_DIM), dt, a_log,
                          bm.reshape(bsz, L, SSD_GROUPS, SSD_STATE),
                          cm.reshape(bsz, L, SSD_GROUPS, SSD_STATE), d_skip, ssm0)
    y = rms_norm(y.astype(h.dtype) * jax.nn.silu(z), norm_g)
    logf = jax.nn.log_sigmoid(f_raw.astype(jnp.float32) + f_bias.astype(jnp.float32))
    q = q.reshape(bsz, L, FOX_HEADS, FOX_HEAD_DIM)
    k = k.reshape(bsz, L, FOX_HEADS, FOX_HEAD_DIM)
    v = v.reshape(bsz, L, FOX_HEADS, FOX_HEAD_DIM)
    k_all = jnp.concatenate([k_past.astype(k.dtype), k], axis=1)
    v_all = jnp.concatenate([v_past.astype(v.dtype), v], axis=1)
    logf_all = jnp.concatenate([logf_past.astype(jnp.float32), logf], axis=1)
    o = fox_attention(q, k_all, v_all, logf_all).reshape(bsz, L, FOX_D)
    out = jnp.concatenate([y, o.astype(y.dtype)], axis=-1) @ w_out
    return out.astype(h.dtype), (k, v, logf, ssm_new, conv_new)


def mixer_sb(h, k_past, v_past, w_qkv, w_out):
    bsz, L, _ = h.shape
    q, k, v = split_sizes(h @ w_qkv, (SB_D, SB_D, SB_D))
    q = q.reshape(bsz, L, SB_HEADS, SB_HEAD_DIM)
    k = k.reshape(bsz, L, SB_HEADS, SB_HEAD_DIM)
    v = v.reshape(bsz, L, SB_HEADS, SB_HEAD_DIM)
    k_all = jnp.concatenate([k_past.astype(k.dtype), k], axis=1)
    v_all = jnp.concatenate([v_past.astype(v.dtype), v], axis=1)
    o = stick_breaking_attention(q, k_all, v_all).reshape(bsz, L, SB_D)
    return (o @ w_out).astype(h.dtype), (k, v)


def memory_kv(mem, g, wk, wv):
    bsz, M, _ = mem.shape
    m = rms_norm(mem, g)
    return ((m @ wk).reshape(bsz, M, CA_HEADS, CA_HEAD_DIM),
            (m @ wv).reshape(bsz, M, CA_HEADS, CA_HEAD_DIM))


def cross_attention(h, mk, mv, wq, wo):
    bsz, L, _ = h.shape
    q = (h @ wq).reshape(bsz, L, CA_HEADS, CA_HEAD_DIM)
    s = jnp.einsum('bqhd,bkhd->bhqk', q, mk).astype(jnp.float32) * (CA_HEAD_DIM ** -0.5)
    p = jax.nn.softmax(s, axis=-1)
    o = jnp.einsum('bhqk,bkhd->bqhd', p.astype(mv.dtype), mv).reshape(bsz, L, CA_D)
    return (o @ wo).astype(h.dtype)


def setup_inputs(seed: int = 0) -> dict:
    key = jax.random.key(seed)
    ks = iter(jax.random.split(key, 64))
    f32 = jnp.float32

    def nrm(shape, scale=1.0):
        return jax.random.normal(next(ks), shape, f32) * scale

    def gain(shape):
        return 1.0 + 0.02 * nrm(shape)

    dt0 = jnp.exp(jax.random.uniform(next(ks), (N_AB, SSD_HEADS), f32, math.log(1e-3), math.log(1e-1)))
    return {
        "x_prompt": nrm((BATCH, SEQ, D_MODEL)),
        "x_sample": nrm((DEC_BATCH, DEC_SEQ, D_MODEL)),
        "cache_fox_k": nrm((N_AB, DEC_BATCH, PAST_LEN, FOX_HEADS, FOX_HEAD_DIM)),
        "cache_fox_v": nrm((N_AB, DEC_BATCH, PAST_LEN, FOX_HEADS, FOX_HEAD_DIM)),
        "cache_fox_logf": jax.nn.log_sigmoid(nrm((N_AB, DEC_BATCH, PAST_LEN, FOX_HEADS)) + 3.0),
        "state_ssm": nrm((N_AB, DEC_BATCH, SSD_HEADS, SSD_HEAD_DIM, SSD_STATE), 0.1),
        "state_conv": nrm((N_AB, DEC_BATCH, CONV_W - 1, CONV_DIM)),
        "cache_sb_k": nrm((N_SB, DEC_BATCH, PAST_LEN, SB_HEADS, SB_HEAD_DIM)),
        "cache_sb_v": nrm((N_SB, DEC_BATCH, PAST_LEN, SB_HEADS, SB_HEAD_DIM)),
        "cache_mem_k": nrm((DEPTH, DEC_BATCH, MEM_LEN, CA_HEADS, CA_HEAD_DIM)),
        "cache_mem_v": nrm((DEPTH, DEC_BATCH, MEM_LEN, CA_HEADS, CA_HEAD_DIM)),
        "mem_prompt": nrm((BATCH, MEM_LEN, D_MODEL)),
        "ffa_norm": gain((DEPTH, D_MODEL)),
        "ffa_wg": nrm((DEPTH, D_MODEL, D_FF), D_MODEL ** -0.5),
        "ffa_wu": nrm((DEPTH, D_MODEL, D_FF), D_MODEL ** -0.5),
        "ffa_wd": nrm((DEPTH, D_FF, D_MODEL), D_FF ** -0.5),
        "mix_norm": gain((DEPTH, D_MODEL)),
        "ab_w_in": nrm((N_AB, D_MODEL, AB_IN), D_MODEL ** -0.5),
        "ab_conv_w": nrm((N_AB, CONV_W, CONV_DIM), CONV_W ** -0.5),
        "ab_conv_b": nrm((N_AB, CONV_DIM), 0.01),
        "ab_dt_bias": dt0 + jnp.log(-jnp.expm1(-dt0)),
        "ab_a_log": jnp.log(jax.random.uniform(next(ks), (N_AB, SSD_HEADS), f32, 1.0, 16.0)),
        "ab_d_skip": 1.0 + 0.1 * nrm((N_AB, SSD_HEADS)),
        "ab_ssd_norm": gain((N_AB, SSD_D)),
        "ab_fox_fb": jax.random.uniform(next(ks), (N_AB, FOX_HEADS), f32, 1.0, 5.0),
        "ab_w_out": nrm((N_AB, AB_MIX, D_MODEL), AB_MIX ** -0.5),
        "sb_w_qkv": nrm((N_SB, D_MODEL, 3 * SB_D), D_MODEL ** -0.5),
        "sb_w_out": nrm((N_SB, SB_D, D_MODEL), SB_D ** -0.5),
        "ca_norm": gain((DEPTH, D_MODEL)),
        "ca_mem_norm": gain((DEPTH, D_MODEL)),
        "ca_wq": nrm((DEPTH, D_MODEL, CA_D), D_MODEL ** -0.5),
        "ca_wk": nrm((DEPTH, D_MODEL, CA_D), D_MODEL ** -0.5),
        "ca_wv": nrm((DEPTH, D_MODEL, CA_D), D_MODEL ** -0.5),
        "ca_wo": nrm((DEPTH, CA_D, D_MODEL), CA_D ** -0.5),
        "ffb_norm": gain((DEPTH, D_MODEL)),
        "ffb_wg": nrm((DEPTH, D_MODEL, D_FF), D_MODEL ** -0.5),
        "ffb_wu": nrm((DEPTH, D_MODEL, D_FF), D_MODEL ** -0.5),
        "ffb_wd": nrm((DEPTH, D_FF, D_MODEL), D_FF ** -0.5),
        "final_norm": gain((D_MODEL,)),
    }


def reference(x_prompt, x_sample, cache_fox_k, cache_fox_v, cache_fox_logf, state_ssm, state_conv,
              cache_sb_k, cache_sb_v, cache_mem_k, cache_mem_v, mem_prompt,
              ffa_norm, ffa_wg, ffa_wu, ffa_wd, mix_norm,
              ab_w_in, ab_conv_w, ab_conv_b, ab_dt_bias, ab_a_log, ab_d_skip, ab_ssd_norm, ab_fox_fb, ab_w_out,
              sb_w_qkv, sb_w_out,
              ca_norm, ca_mem_norm, ca_wq, ca_wk, ca_wv, ca_wo,
              ffb_norm, ffb_wg, ffb_wu, ffb_wd, final_norm):

    def trunk(x, fox_k, fox_v, fox_logf, ssm, conv, sb_k, sb_v, mem_k, mem_v):
        ab_states, sb_states = [], []
        for l in range(DEPTH):
            i = l // 2
            h = x + FFN_HALF * swiglu(rms_norm(x, ffa_norm[l]), ffa_wg[l], ffa_wu[l], ffa_wd[l])
            hn = rms_norm(h, mix_norm[l])
            if l % 2 == 0:
                out, st = mixer_ab(hn, conv[i], ssm[i], fox_k[i], fox_v[i], fox_logf[i],
                                   ab_w_in[i], ab_conv_w[i], ab_conv_b[i], ab_dt_bias[i], ab_a_log[i],
                                   ab_d_skip[i], ab_ssd_norm[i], ab_fox_fb[i], ab_w_out[i])
                ab_states.append(st)
            else:
                out, st = mixer_sb(hn, sb_k[i], sb_v[i], sb_w_qkv[i], sb_w_out[i])
                sb_states.append(st)
            h = h + out
            h = h + cross_attention(rms_norm(h, ca_norm[l]), mem_k[l], mem_v[l], ca_wq[l], ca_wo[l])
            x = h + FFN_HALF * swiglu(rms_norm(h, ffb_norm[l]), ffb_wg[l], ffb_wu[l], ffb_wd[l])
        y = rms_norm(x, final_norm)
        ab_new = [jnp.stack([st[j] for st in ab_states]) for j in range(5)]
        sb_new = [jnp.stack([st[j] for st in sb_states]) for j in range(2)]
        return y, ab_new, sb_new

    bp, dtp = x_prompt.shape[0], x_prompt.dtype
    mem_kv_p = [memory_kv(mem_prompt, ca_mem_norm[l], ca_wk[l], ca_wv[l]) for l in range(DEPTH)]
    mem_k_p = jnp.stack([kv[0] for kv in mem_kv_p])
    mem_v_p = jnp.stack([kv[1] for kv in mem_kv_p])
    y_prompt, ab_p, sb_p = trunk(
        x_prompt,
        jnp.zeros((N_AB, bp, 0, FOX_HEADS, FOX_HEAD_DIM), dtp),
        jnp.zeros((N_AB, bp, 0, FOX_HEADS, FOX_HEAD_DIM), dtp),
        jnp.zeros((N_AB, bp, 0, FOX_HEADS), jnp.float32),
        jnp.zeros((N_AB, bp, SSD_HEADS, SSD_HEAD_DIM, SSD_STATE), dtp),
        jnp.zeros((N_AB, bp, CONV_W - 1, CONV_DIM), dtp),
        jnp.zeros((N_SB, bp, 0, SB_HEADS, SB_HEAD_DIM), dtp),
        jnp.zeros((N_SB, bp, 0, SB_HEADS, SB_HEAD_DIM), dtp),
        mem_k_p, mem_v_p)
    fox_k_p, fox_v_p, fox_logf_p, ssm_p, conv_p = ab_p
    sb_k_p, sb_v_p = sb_p

    y_sample, ab_s, sb_s = trunk(x_sample, cache_fox_k, cache_fox_v, cache_fox_logf, state_ssm, state_conv,
                                 cache_sb_k, cache_sb_v, cache_mem_k, cache_mem_v)
    fox_k_s, fox_v_s, fox_logf_s, ssm_s, conv_s = ab_s
    sb_k_s, sb_v_s = sb_s

    return (y_prompt, y_sample,
            fox_k_p, fox_v_p, fox_logf_p, ssm_p, conv_p, sb_k_p, sb_v_p, mem_k_p, mem_v_p,
            fox_k_s, fox_v_s, fox_logf_s, ssm_s, conv_s, sb_k_s, sb_v_s)
```

```python
import functools

import jax
import jax.numpy as jnp
from jax import lax
from jax.experimental import pallas as pl
from jax.experimental.pallas import tpu as pltpu

F32 = jnp.float32
BF16 = jnp.bfloat16

RMS_EPS = 1e-6
FFN_HALF = 0.5
CONV_W = 4
SSD_HEAD_DIM = 64
SSD_GROUPS = 4
SSD_STATE = 128
FOX_HEAD_DIM = 128
SB_HEAD_DIM = 128
CA_HEADS = 4
CHUNK = 64

LANES = 128
SUBLANES = 8
VMEM_LIMIT_BYTES = 56 * 1024 * 1024
NEG_BIG = -1e30


def _params(*sem):
    return pltpu.CompilerParams(dimension_semantics=sem, vmem_limit_bytes=VMEM_LIMIT_BYTES)


def _tile(n, pref, mult=SUBLANES):
    if n <= pref:
        return n
    for t in range(pref, 0, -1):
        if n % t == 0 and t % mult == 0:
            return t
    return n


def _split3(a):
    hi = a.astype(BF16)
    r = a - hi.astype(F32)
    mid = r.astype(BF16)
    lo = (r - mid.astype(F32)).astype(BF16)
    return hi, mid, lo


def _dot(a, b):
    return jnp.dot(a, b, preferred_element_type=F32)


def _dot_nt(a, b):
    return lax.dot_general(a, b, (((1,), (1,)), ((), ())), preferred_element_type=F32)


def _dot_tn(a, b):
    return lax.dot_general(a, b, (((0,), (0,)), ((), ())), preferred_element_type=F32)


def _rms(x, g):
    ms = jnp.mean(x * x, axis=-1, keepdims=True)
    return x * lax.rsqrt(ms + RMS_EPS) * g


def _silu(x):
    return x * jax.nn.sigmoid(x)


def _norm_matmul_kernel(x_ref, g_ref, w_ref, *refs, segs):
    o_refs, n_ref = refs[:-1], refs[-1]
    j = pl.program_id(1)

    @pl.when(j == 0)
    def _():
        n_ref[...] = _rms(x_ref[...], g_ref[...]).astype(n_ref.dtype)

    r = _dot(n_ref[...], w_ref[...])
    if len(o_refs) == 1:
        o_refs[0][...] = r.astype(o_refs[0].dtype)
        return
    for o_ref, (lo, hi) in zip(o_refs, segs):
        @pl.when((j >= lo) & (j < hi))
        def _(o_ref=o_ref):
            o_ref[...] = r.astype(o_ref.dtype)


def norm_matmul(x, g, w, widths, dtypes, *, tm_pref=512, tn_pref=512):
    T, D = x.shape
    N = w.shape[1]
    assert sum(widths) == N
    tm = _tile(T, tm_pref)
    tn = min(tn_pref, min(widths))
    assert all(wd % tn == 0 for wd in widths)
    segs, lo = [], 0
    for wd in widths:
        segs.append((lo, lo + wd // tn))
        lo += wd // tn

    def out_map(lo, n):
        return lambda i, j: (i, jnp.clip(j - lo, 0, n - 1))

    return pl.pallas_call(
        functools.partial(_norm_matmul_kernel, segs=tuple(segs)),
        grid=(T // tm, N // tn),
        in_specs=[pl.BlockSpec((tm, D), lambda i, j: (i, 0)),
                  pl.BlockSpec((1, D), lambda i, j: (0, 0)),
                  pl.BlockSpec((D, tn), lambda i, j: (0, j))],
        out_specs=[pl.BlockSpec((tm, tn), out_map(lo, hi - lo)) for lo, hi in segs],
        out_shape=[jax.ShapeDtypeStruct((T, wd), dt) for wd, dt in zip(widths, dtypes)],
        scratch_shapes=[pltpu.VMEM((tm, D), BF16)],
        compiler_params=_params("parallel", "arbitrary"),
    )(x, g.reshape(1, D), w)


def _ffn_kernel(x_ref, g_ref, wg_ref, wu_ref, wd_ref, fg_ref, o_ref, n_ref, *, final_norm):
    j = pl.program_id(1)

    @pl.when(j == 0)
    def _():
        n_ref[...] = _rms(x_ref[...], g_ref[...]).astype(n_ref.dtype)

    n = n_ref[...]
    hidden = (_silu(_dot(n, wg_ref[...])) * _dot(n, wu_ref[...])).astype(BF16)
    part = _dot(hidden, wd_ref[...])

    @pl.when(j == 0)
    def _():
        o_ref[...] = part

    @pl.when(j > 0)
    def _():
        o_ref[...] += part

    @pl.when(j == pl.num_programs(1) - 1)
    def _():
        y = x_ref[...] + FFN_HALF * o_ref[...]
        if final_norm:
            y = _rms(y, fg_ref[...])
        o_ref[...] = y


def ffn(x, g, wg, wu, wd, fg=None, *, tm_pref=512, tf_pref=512):
    T, D = x.shape
    FF = wg.shape[1]
    tm = _tile(T, tm_pref)
    tf = _tile(FF, tf_pref, LANES)
    final_norm = fg is not None
    fg = g if fg is None else fg
    return pl.pallas_call(
        functools.partial(_ffn_kernel, final_norm=final_norm),
        grid=(T // tm, FF // tf),
        in_specs=[pl.BlockSpec((tm, D), lambda i, j: (i, 0)),
                  pl.BlockSpec((1, D), lambda i, j: (0, 0)),
                  pl.BlockSpec((D, tf), lambda i, j: (0, j)),
                  pl.BlockSpec((D, tf), lambda i, j: (0, j)),
                  pl.BlockSpec((tf, D), lambda i, j: (j, 0)),
                  pl.BlockSpec((1, D), lambda i, j: (0, 0))],
        out_specs=pl.BlockSpec((tm, D), lambda i, j: (i, 0)),
        out_shape=jax.ShapeDtypeStruct((T, D), F32),
        scratch_shapes=[pltpu.VMEM((tm, D), BF16)],
        compiler_params=_params("parallel", "arbitrary"),
    )(x, g.reshape(1, D), wg, wu, wd, fg.reshape(1, D))


def _matmul_res_kernel(res_ref, *refs):
    o_ref = refs[-1]
    acc = res_ref[...]
    for p in range((len(refs) - 1) // 2):
        acc = acc + _dot(refs[2 * p][...], refs[2 * p + 1][...])
    o_ref[...] = acc


def matmul_res(res, pairs, *, tm_pref=512, tn_pref=1024):
    T, D = res.shape
    tm = _tile(T, tm_pref)
    tn = _tile(D, tn_pref, LANES)
    in_specs = [pl.BlockSpec((tm, tn), lambda i, j: (i, j))]
    args = [res]
    for a, w in pairs:
        K = a.shape[1]
        in_specs += [pl.BlockSpec((tm, K), lambda i, j: (i, 0)), pl.BlockSpec((K, tn), lambda i, j: (0, j))]
        args += [a, w]
    return pl.pallas_call(
        _matmul_res_kernel,
        grid=(T // tm, D // tn),
        in_specs=in_specs,
        out_specs=pl.BlockSpec((tm, tn), lambda i, j: (i, j)),
        out_shape=jax.ShapeDtypeStruct((T, D), F32),
        compiler_params=_params("parallel", "parallel"),
    )(*args)


def _cumsum_kernel(x_ref, b_ref, init_ref, v_ref, c_ref, carry_ref, *, log_sigmoid):
    @pl.when(pl.program_id(1) == 0)
    def _():
        carry_ref[...] = init_ref[0]

    x = x_ref[0]
    if log_sigmoid:
        x = jax.nn.log_sigmoid(x + b_ref[...])
    tb = x.shape[0]
    row = lax.broadcasted_iota(jnp.int32, (tb, tb), 0)
    col = lax.broadcasted_iota(jnp.int32, (tb, tb), 1)
    tri = jnp.where(row >= col, 1.0, 0.0).astype(BF16)
    hi, mid, lo = _split3(x)
    c = carry_ref[...] + ((_dot(tri, hi) + _dot(tri, mid)) + _dot(tri, lo))
    v_ref[0] = x
    c_ref[0] = c
    carry_ref[...] = c[tb - 1:tb, :]


def seq_cumsum(x, bias, init, *, log_sigmoid, tb_pref=256):
    B, L, W = x.shape
    tb = _tile(L, tb_pref)
    return pl.pallas_call(
        functools.partial(_cumsum_kernel, log_sigmoid=log_sigmoid),
        grid=(B, L // tb),
        in_specs=[pl.BlockSpec((1, tb, W), lambda b, i: (b, i, 0)),
                  pl.BlockSpec((1, W), lambda b, i: (0, 0)),
                  pl.BlockSpec((1, 1, W), lambda b, i: (b, 0, 0))],
        out_specs=[pl.BlockSpec((1, tb, W), lambda b, i: (b, i, 0)),
                   pl.BlockSpec((1, tb, W), lambda b, i: (b, i, 0))],
        out_shape=[jax.ShapeDtypeStruct((B, L, W), F32), jax.ShapeDtypeStruct((B, L, W), F32)],
        scratch_shapes=[pltpu.VMEM((1, W), F32)],
        compiler_params=_params("parallel", "arbitrary"),
    )(x, bias.reshape(1, W), init)


def _ssd_kernel(z_ref, xbc_ref, dt_ref, ctx0_ref, s0_ref, cw_ref, cb_ref, dtb_ref, alog_ref, dsk_ref,
                ng_ref, y_ref, sout_ref, up_ref, st_ref, *, S, H, G, P, N):
    c = pl.program_id(1)
    XD, GN = H * P, G * N
    hpg = H // G
    hb = LANES // S
    pad = SUBLANES - (CONV_W - 1)

    @pl.when(c == 0)
    def _():
        up_ref[pad:SUBLANES, :] = ctx0_ref[0]
        st_ref[...] = s0_ref[0].T

    up_ref[SUBLANES:SUBLANES + S, :] = xbc_ref[0]
    conv = up_ref[pad:pad + S, :] * cw_ref[0:1, :]
    for k in range(1, CONV_W):
        conv = conv + up_ref[pad + k:pad + k + S, :] * cw_ref[k:k + 1, :]
    conv = conv + cb_ref[...]
    up_ref[pad:SUBLANES, :] = up_ref[pad + S:SUBLANES + S, :]
    xbc = _silu(conv)
    xs, bm, cm = xbc[:, :XD], xbc[:, XD:XD + GN], xbc[:, XD + GN:]

    dt = jax.nn.softplus(dt_ref[0][:, :H] + dtb_ref[...])
    dta = dt * (-jnp.exp(alog_ref[...]))
    row = lax.broadcasted_iota(jnp.int32, (S, S), 0)
    col = lax.broadcasted_iota(jnp.int32, (S, S), 1)
    tri = jnp.where(row >= col, 1.0, 0.0).astype(BF16)
    hi, mid, lo = _split3(dta)
    acum = (_dot(tri, hi) + _dot(tri, mid)) + _dot(tri, lo)
    last = acum[S - 1:S, :]

    def expand(a, width):
        hrow = lax.broadcasted_iota(jnp.int32, (H, H * width), 0)
        hcol = lax.broadcasted_iota(jnp.int32, (H, H * width), 1) // width
        e = jnp.where(hrow == hcol, 1.0, 0.0).astype(BF16)
        p0, p1, p2 = _split3(a)
        return (_dot(p0, e) + _dot(p1, e)) + _dot(p2, e)

    rows8 = lambda v: jnp.broadcast_to(v, (SUBLANES, H))
    per_p = expand(jnp.concatenate([jnp.exp(last - acum) * dt, jnp.exp(acum),
                                    rows8(jnp.exp(last)), rows8(dsk_ref[...])], axis=0), P)
    wst_full, eac_full = per_p[:S], per_p[S:2 * S]
    cd_full, dsk_full = per_p[2 * S:2 * S + 1], per_p[2 * S + SUBLANES:2 * S + SUBLANES + 1]
    per_s = expand(jnp.concatenate([acum, dt], axis=0), S)
    acum_full, dt_full = per_s[:S], per_s[S:]
    lrow = lax.broadcasted_iota(jnp.int32, (S, H * S), 0)
    scol = lax.broadcasted_iota(jnp.int32, (S, H * S), 1) % S
    diag = lrow == scol
    acum_row = jnp.sum(jnp.where(diag, acum_full, 0.0), axis=0, keepdims=True)
    dt_row = jnp.sum(jnp.where(diag, dt_full, 0.0), axis=0, keepdims=True)

    xs_b, bm_b, cm_b = xs.astype(BF16), bm.astype(BF16), cm.astype(BF16)
    st = st_ref[...]
    st_b = st.astype(BF16)

    causal = lax.broadcasted_iota(jnp.int32, (S, LANES), 0) >= lax.broadcasted_iota(jnp.int32, (S, LANES), 1) % S
    bd_row = lax.broadcasted_iota(jnp.int32, (LANES, hb * P), 0) // S
    bd_col = lax.broadcasted_iota(jnp.int32, (LANES, hb * P), 1) // P
    block_diag = bd_row == bd_col
    cb_rep = []
    for g in range(G):
        b_rep = jnp.concatenate([bm_b[:, g * N:(g + 1) * N]] * hb, axis=0)
        cb_rep.append(_dot_nt(cm_b[:, g * N:(g + 1) * N], b_rep))
    y_parts = []
    for blk in range(H // hb):
        g = (blk * hb) // hpg
        ln = slice(blk * LANES, (blk + 1) * LANES)
        seg = acum_full[:, ln] - acum_row[:, ln]
        decay = jnp.exp(jnp.where(causal, seg, -jnp.inf))
        w = (cb_rep[g] * decay * dt_row[:, ln]).astype(BF16)
        x_blk = xs_b[:, blk * hb * P:(blk + 1) * hb * P]
        x_bd = jnp.where(block_diag, jnp.concatenate([x_blk] * hb, axis=0), jnp.zeros((), BF16))
        y_parts.append(_dot(w, x_bd))
    y_diag = jnp.concatenate(y_parts, axis=1)

    y_off = jnp.concatenate([_dot(cm_b[:, g * N:(g + 1) * N], st_b[:, g * hpg * P:(g + 1) * hpg * P])
                             for g in range(G)], axis=1)
    xw = (xs * wst_full).astype(BF16)
    st_new = jnp.concatenate([_dot_tn(bm_b[:, g * N:(g + 1) * N], xw[:, g * hpg * P:(g + 1) * hpg * P])
                              for g in range(G)], axis=1)
    st_ref[...] = cd_full * st + st_new

    y = y_diag + y_off * eac_full + dsk_full * xs
    y_ref[0] = _rms(y * _silu(z_ref[0]), ng_ref[...]).astype(y_ref.dtype)

    @pl.when(c == pl.num_programs(1) - 1)
    def _():
        sout_ref[0] = st_ref[...].T


def ssd(z, xbc, dtf, ctx0, s0, conv_w, conv_b, dt_bias, a_log, d_skip, norm_g):
    B, L, XD = z.shape
    CD = xbc.shape[2]
    H = XD // SSD_HEAD_DIM
    S = min(CHUNK, L)
    assert LANES % S == 0 and (H // SSD_GROUPS) % (LANES // S) == 0 and L % S == 0
    row = lambda v: v.reshape(1, -1)
    const = lambda shape: pl.BlockSpec(shape, lambda b, c: (0,) * len(shape))
    return pl.pallas_call(
        functools.partial(_ssd_kernel, S=S, H=H, G=SSD_GROUPS, P=SSD_HEAD_DIM, N=SSD_STATE),
        grid=(B, L // S),
        in_specs=[pl.BlockSpec((1, S, XD), lambda b, c: (b, c, 0)),
                  pl.BlockSpec((1, S, CD), lambda b, c: (b, c, 0)),
                  pl.BlockSpec((1, S, dtf.shape[2]), lambda b, c: (b, c, 0)),
                  pl.BlockSpec((1, CONV_W - 1, CD), lambda b, c: (b, 0, 0)),
                  pl.BlockSpec((1, XD, SSD_STATE), lambda b, c: (b, 0, 0)),
                  const((CONV_W, CD)), const((1, CD)), const((1, H)), const((1, H)), const((1, H)),
                  const((1, XD))],
        out_specs=[pl.BlockSpec((1, S, XD), lambda b, c: (b, c, 0)),
                   pl.BlockSpec((1, XD, SSD_STATE), lambda b, c: (b, 0, 0))],
        out_shape=[jax.ShapeDtypeStruct((B, L, XD), BF16), jax.ShapeDtypeStruct((B, XD, SSD_STATE), F32)],
        scratch_shapes=[pltpu.VMEM((S + SUBLANES, CD), F32), pltpu.VMEM((SSD_STATE, XD), F32)],
        compiler_params=_params("parallel", "arbitrary"),
    )(z, xbc, dtf, ctx0, s0, conv_w, row(conv_b), row(dt_bias), row(a_log), row(d_skip), row(norm_g))


def _fox_kernel(*refs, tq, nq, tkp, n_past, scale):
    if n_past:
        q_ref, cq_ref, kn_ref, vn_ref, ckn_ref, kp_ref, vp_ref, ckp_ref, o_ref, m_ref, l_ref, acc_ref = refs
    else:
        q_ref, cq_ref, kn_ref, vn_ref, ckn_ref, o_ref, m_ref, l_ref, acc_ref = refs
    h, i = pl.program_id(1), pl.program_id(2)
    q = (q_ref[0] * scale).astype(BF16)
    cq_all = cq_ref[0]
    head = lax.broadcasted_iota(jnp.int32, cq_all.shape, 1)
    cq = jnp.sum(jnp.where(head == h, cq_all, 0.0), axis=1, keepdims=True)
    m_ref[...] = jnp.full(m_ref.shape, NEG_BIG, F32)
    l_ref[...] = jnp.zeros(l_ref.shape, F32)
    acc_ref[...] = jnp.zeros(acc_ref.shape, F32)

    def update(k, v, ck, mask):
        s = _dot_nt(q, k.astype(BF16)) + (cq - ck)
        if mask is not None:
            s = jnp.where(mask, s, -jnp.inf)
        m_old = m_ref[...]
        m_new = jnp.maximum(m_old, jnp.max(s, axis=1, keepdims=True))
        alpha = jnp.exp(m_old - m_new)
        p = jnp.exp(s - m_new)
        l_ref[...] = alpha * l_ref[...] + jnp.sum(p, axis=1, keepdims=True)
        acc_ref[...] = alpha * acc_ref[...] + _dot(p.astype(BF16), v.astype(BF16))
        m_ref[...] = m_new

    if n_past:
        def past_step(j, carry):
            st = pl.multiple_of(j * tkp, tkp)
            update(kp_ref[0, pl.ds(st, tkp), :], vp_ref[0, pl.ds(st, tkp), :],
                   ckp_ref[0, pl.ds(h, 1), pl.ds(st, tkp)], None)
            return carry
        lax.fori_loop(0, n_past, past_step, 0)

    def new_step(j, carry):
        st = pl.multiple_of(j * tq, tq)
        update(kn_ref[0, pl.ds(st, tq), :], vn_ref[0, pl.ds(st, tq), :],
               ckn_ref[0, pl.ds(h, 1), pl.ds(st, tq)], None)
        return carry
    if nq > 1:
        lax.fori_loop(0, i, new_step, 0)

    st = pl.multiple_of(i * tq, tq) if nq > 1 else 0
    causal =lax.broadcasted_iota(jnp.int32, (tq, tq), 0) >= lax.broadcasted_iota(jnp.int32, (tq, tq), 1)
    update(kn_ref[0, pl.ds(st, tq), :], vn_ref[0, pl.ds(st, tq), :],
           ckn_ref[0, pl.ds(h, 1), pl.ds(st, tq)], causal)
    o_ref[0] = (acc_ref[...] / l_ref[...]).astype(o_ref.dtype)


def fox_attention(q, k, v, cq, ck, past=None, *, tq_pref=256, tkp_pref=512):
    B, L, HD = q.shape
    d = FOX_HEAD_DIM
    H = HD // d
    tq = _tile(L, tq_pref)
    in_specs = [pl.BlockSpec((1, tq, d), lambda b, h, i: (b, i, h)),
                pl.BlockSpec((1, tq, H), lambda b, h, i: (b, i, 0)),
                pl.BlockSpec((1, L, d), lambda b, h, i: (b, 0, h)),
                pl.BlockSpec((1, L, d), lambda b, h, i: (b, 0, h)),
                pl.BlockSpec((1, H, L), lambda b, h, i: (b, 0, 0))]
    args = [q, cq, k, v, ck]
    tkp = n_past = 0
    if past is not None:
        kp, vp, ckp = past
        Lp = kp.shape[1]
        tkp = _tile(Lp, tkp_pref, LANES)
        n_past = Lp // tkp
        in_specs += [pl.BlockSpec((1, Lp, d), lambda b, h, i: (b, 0, h)),
                     pl.BlockSpec((1, Lp, d), lambda b, h, i: (b, 0, h)),
                     pl.BlockSpec((1, H, Lp), lambda b, h, i: (b, 0, 0))]
        args += [kp, vp, ckp]
    return pl.pallas_call(
        functools.partial(_fox_kernel, tq=tq, nq=L // tq, tkp=tkp, n_past=n_past, scale=d ** -0.5),
        grid=(B, H, L // tq),
        in_specs=in_specs,
        out_specs=pl.BlockSpec((1, tq, d), lambda b, h, i: (b, i, h)),
        out_shape=jax.ShapeDtypeStruct((B, L, HD), BF16),
        scratch_shapes=[pltpu.VMEM((tq, 1), F32), pltpu.VMEM((tq, 1), F32), pltpu.VMEM((tq, d), F32)],
        compiler_params=_params("parallel", "parallel", "arbitrary"),
    )(*args)


def _sb_kernel(*refs, tq, nq, tkp, n_past, scale):
    if n_past:
        q_ref, kn_ref, vn_ref, un_ref, kp_ref, vp_ref, up_ref, o_ref, r_ref, acc_ref = refs
    else:
        q_ref, kn_ref, vn_ref, un_ref, o_ref, r_ref, acc_ref = refs
    i = pl.program_id(2)
    q = (q_ref[0] * scale).astype(BF16)
    r_ref[...] = jnp.zeros(r_ref.shape, F32)
    acc_ref[...] = jnp.zeros(acc_ref.shape, F32)

    def update(k, v, u, mask):
        z = _dot_nt(q, k.astype(BF16))
        log_keep = jnp.minimum(-z, 0.0) - jnp.log1p(jnp.exp(-jnp.abs(z)))
        if mask is not None:
            log_keep = jnp.where(mask, log_keep, 0.0)
        hi = log_keep.astype(BF16)
        lo = (log_keep - hi.astype(F32)).astype(BF16)
        suffix = _dot(hi, u) + _dot(lo, u)
        a = jnp.exp(z + suffix + r_ref[...])
        if mask is not None:
            a = jnp.where(mask, a, 0.0)
        acc_ref[...] += _dot(a.astype(BF16), v.astype(BF16))
        r_ref[...] += suffix[:, 0:1]

    st = pl.multiple_of(i * tq, tq) if nq > 1 else 0
    strict =lax.broadcasted_iota(jnp.int32, (tq, tq), 0) > lax.broadcasted_iota(jnp.int32, (tq, tq), 1)
    update(kn_ref[0, pl.ds(st, tq), :], vn_ref[0, pl.ds(st, tq), :], un_ref[...], strict)

    def new_step(jj, carry):
        st = pl.multiple_of((i - 1 - jj) * tq, tq)
        update(kn_ref[0, pl.ds(st, tq), :], vn_ref[0, pl.ds(st, tq), :], un_ref[...], None)
        return carry
    if nq > 1:
        lax.fori_loop(0, i, new_step, 0)

    if n_past:
        def past_step(jj, carry):
            st = pl.multiple_of((n_past - 1 - jj) * tkp, tkp)
            update(kp_ref[0, pl.ds(st, tkp), :], vp_ref[0, pl.ds(st, tkp), :], up_ref[...], None)
            return carry
        lax.fori_loop(0, n_past, past_step, 0)
    o_ref[0] = acc_ref[...].astype(o_ref.dtype)


def _suffix_matrix(t):
    j = lax.broadcasted_iota(jnp.int32, (t, t), 0)
    s = lax.broadcasted_iota(jnp.int32, (t, t), 1)
    return jnp.where(j >= s, 1.0, 0.0).astype(BF16)


def sb_attention(q, k, v, past=None, *, tq_pref=256, tkp_pref=512):
    B, L, HD = q.shape
    d = SB_HEAD_DIM
    H = HD // d
    tq = _tile(L, tq_pref)
    in_specs = [pl.BlockSpec((1, tq, d), lambda b, h, i: (b, i, h)),
                pl.BlockSpec((1, L, d), lambda b, h, i: (b, 0, h)),
                pl.BlockSpec((1, L, d), lambda b, h, i: (b, 0, h)),
                pl.BlockSpec((tq, tq), lambda b, h, i: (0, 0))]
    args = [q, k, v, _suffix_matrix(tq)]
    tkp = n_past = 0
    if past is not None:
        kp, vp = past
        Lp = kp.shape[1]
        tkp = _tile(Lp, tkp_pref, LANES)
        n_past = Lp // tkp
        in_specs += [pl.BlockSpec((1, Lp, d), lambda b, h, i: (b, 0, h)),
                     pl.BlockSpec((1, Lp, d), lambda b, h, i: (b, 0, h)),
                     pl.BlockSpec((tkp, tkp), lambda b, h, i: (0, 0))]
        args += [kp, vp, _suffix_matrix(tkp)]
    return pl.pallas_call(
        functools.partial(_sb_kernel, tq=tq, nq=L // tq, tkp=tkp, n_past=n_past, scale=d ** -0.5),
        grid=(B, H, L // tq),
        in_specs=in_specs,
        out_specs=pl.BlockSpec((1, tq, d), lambda b, h, i: (b, i, h)),
        out_shape=jax.ShapeDtypeStruct((B, L, HD), BF16),
        scratch_shapes=[pltpu.VMEM((tq, 1), F32), pltpu.VMEM((tq, d), F32)],
        compiler_params=_params("parallel", "parallel", "arbitrary"),
    )(*args)


def _cross_kernel(q_ref, k_ref, v_ref, o_ref, *, heads, scale):
    dh = q_ref.shape[2] // heads
    for hd in range(heads):
        cols = slice(hd * dh, (hd + 1) * dh)
        s = _dot_nt(q_ref[0, :, cols], k_ref[0, :, cols].astype(BF16)) * scale
        s = s - jnp.max(s, axis=1, keepdims=True)
        p = jnp.exp(s)
        p = p / jnp.sum(p, axis=1, keepdims=True)
        o_ref[0, :, cols] = _dot(p.astype(BF16), v_ref[0, :, cols].astype(BF16)).astype(o_ref.dtype)


def cross_attention_core(q, mk, mv, *, tq_pref=512):
    B, L, D = q.shape
    M = mk.shape[1]
    tq = _tile(L, tq_pref)
    return pl.pallas_call(
        functools.partial(_cross_kernel, heads=CA_HEADS, scale=(D // CA_HEADS) ** -0.5),
        grid=(B, L // tq),
        in_specs=[pl.BlockSpec((1, tq, D), lambda b, i: (b, i, 0)),
                  pl.BlockSpec((1, M, D), lambda b, i: (b, 0, 0)),
                  pl.BlockSpec((1, M, D), lambda b, i: (b, 0, 0))],
        out_specs=pl.BlockSpec((1, tq, D), lambda b, i: (b, i, 0)),
        out_shape=jax.ShapeDtypeStruct((B, L, D), BF16),
        compiler_params=_params("parallel", "parallel"),
    )(q, mk, mv)


def _trunk(x, fox_past, ssm0, conv0, sb_past, mem_k, mem_v, W):
    B, L, D = x.shape
    T = B * L
    as3 = lambda a: a.reshape(B, L, a.shape[-1])
    h = x.reshape(T, D)
    fox_heads = D // FOX_HEAD_DIM
    ssd_heads = D // SSD_HEAD_DIM
    conv_dim = D + 2 * SSD_GROUPS * SSD_STATE

    h = ffn(h, W["ffa_norm"][0], *W["ffa"][0])
    z, xbc, q, k, v = norm_matmul(h, W["mix_norm"][0], W["ab_w_main"], (D, conv_dim, D, D, D), (F32,) * 5)
    (dtf,) = norm_matmul(h, W["mix_norm"][0], W["ab_w_dtf"], (LANES,), (F32,))
    f_raw = as3(dtf)[:, :, ssd_heads:ssd_heads + fox_heads]
    zero_init = jnp.zeros((B, 1, fox_heads), F32)
    if fox_past is None:
        logf, c_new = seq_cumsum(f_raw, W["ab_fox_fb"], zero_init, log_sigmoid=True)
        past = None
    else:
        kp, vp, logf_p = fox_past
        _, c_past = seq_cumsum(logf_p, W["ab_fox_fb"], zero_init, log_sigmoid=False)
        logf, c_new = seq_cumsum(f_raw, W["ab_fox_fb"], c_past[:, -1:, :], log_sigmoid=True)
        past = (kp, vp, c_past.transpose(0, 2, 1))
    y_ssd, ssm_new = ssd(as3(z), as3(xbc), as3(dtf), conv0, ssm0, W["ab_conv_w"], W["ab_conv_b"],
                         W["ab_dt_bias"], W["ab_a_log"], W["ab_d_skip"], W["ab_ssd_norm"])
    conv_new = as3(xbc)[:, L - (CONV_W - 1):, :]
    o = fox_attention(as3(q), as3(k), as3(v), c_new, c_new.transpose(0, 2, 1), past)
    h = matmul_res(h, [(y_ssd.reshape(T, D), W["ab_w_out_y"]), (o.reshape(T, D), W["ab_w_out_o"])])
    ab_state = (k, v, logf, ssm_new, conv_new)

    def cross(h, l):
        (qc,) = norm_matmul(h, W["ca_norm"][l], W["ca_wq"][l], (D,), (BF16,))
        oc = cross_attention_core(as3(qc), mem_k[l], mem_v[l])
        return matmul_res(h, [(oc.reshape(T, D), W["ca_wo"][l])])

    h = cross(h, 0)
    h = ffn(h, W["ffb_norm"][0], *W["ffb"][0])

    h = ffn(h, W["ffa_norm"][1], *W["ffa"][1])
    q, k2, v2 = norm_matmul(h, W["mix_norm"][1], W["sb_w_qkv"], (D, D, D), (F32,) * 3)
    o = sb_attention(as3(q), as3(k2), as3(v2), sb_past)
    h = matmul_res(h, [(o.reshape(T, D), W["sb_w_out"])])
    h = cross(h, 1)
    y = ffn(h, W["ffb_norm"][1], *W["ffb"][1], fg=W["final_norm"])
    return y.reshape(B, L, D), ab_state, (k2, v2)


def kernel(x_prompt, x_sample, cache_fox_k, cache_fox_v, cache_fox_logf, state_ssm, state_conv, cache_sb_k, cache_sb_v, cache_mem_k, cache_mem_v, mem_prompt, ffa_norm, ffa_wg, ffa_wu, ffa_wd, mix_norm, ab_w_in, ab_conv_w, ab_conv_b, ab_dt_bias, ab_a_log, ab_d_skip, ab_ssd_norm, ab_fox_fb, ab_w_out, sb_w_qkv, sb_w_out, ca_norm, ca_mem_norm, ca_wq, ca_wk, ca_wv, ca_wo, ffb_norm, ffb_wg, ffb_wu, ffb_wd, final_norm):
    BP, LP, D = x_prompt.shape
    BS, LS, _ = x_sample.shape
    depth = ffa_norm.shape[0]
    assert depth == 2 and ab_w_in.shape[0] == 1 and sb_w_qkv.shape[0] == 1
    ssd_heads, fox_heads = D // SSD_HEAD_DIM, D // FOX_HEAD_DIM
    conv_dim = D + 2 * SSD_GROUPS * SSD_STATE
    M = mem_prompt.shape[1]
    bf = lambda a: a.astype(BF16)

    w_in = ab_w_in[0]
    c0 = D + conv_dim
    c1 = c0 + ssd_heads
    c2 = c1 + 3 * D
    w_dtf = jnp.concatenate([w_in[:, c0:c1], w_in[:, c2:]], axis=1)
    W = {
        "ffa_norm": ffa_norm, "ffb_norm": ffb_norm, "mix_norm": mix_norm, "ca_norm": ca_norm,
        "ffa": [(bf(ffa_wg[l]), bf(ffa_wu[l]), bf(ffa_wd[l])) for l in range(depth)],
        "ffb": [(bf(ffb_wg[l]), bf(ffb_wu[l]), bf(ffb_wd[l])) for l in range(depth)],
        "ab_w_main": bf(jnp.concatenate([w_in[:, :c0], w_in[:, c1:c2]], axis=1)),
        "ab_w_dtf": bf(jnp.pad(w_dtf, ((0, 0), (0, LANES - w_dtf.shape[1])))),
        "ab_conv_w": ab_conv_w[0], "ab_conv_b": ab_conv_b[0], "ab_dt_bias": ab_dt_bias[0],
        "ab_a_log": ab_a_log[0], "ab_d_skip": ab_d_skip[0], "ab_ssd_norm": ab_ssd_norm[0],
        "ab_fox_fb": ab_fox_fb[0],
        "ab_w_out_y": bf(ab_w_out[0, :D]), "ab_w_out_o": bf(ab_w_out[0, D:]),
        "sb_w_qkv": bf(sb_w_qkv[0]), "sb_w_out": bf(sb_w_out[0]),
        "ca_wq": [bf(ca_wq[l]) for l in range(depth)], "ca_wo": [bf(ca_wo[l]) for l in range(depth)],
        "final_norm": final_norm,
    }

    mem2 = mem_prompt.reshape(BP * M, D)
    mem_kv = [norm_matmul(mem2, ca_mem_norm[l], bf(jnp.concatenate([ca_wk[l], ca_wv[l]], axis=1)),
                          (D, D), (F32, F32)) for l in range(depth)]
    mem_k_p = jnp.stack([kv[0].reshape(BP, M, D) for kv in mem_kv])
    mem_v_p = jnp.stack([kv[1].reshape(BP, M, D) for kv in mem_kv])

    y_p, ab_p, sb_p = _trunk(
        x_prompt, None, jnp.zeros((BP, D, SSD_STATE), F32), jnp.zeros((BP, CONV_W - 1, conv_dim), F32),
        None, mem_k_p, mem_v_p, W)

    past_len = cache_fox_k.shape[2]
    flat = lambda a: a.reshape(BS, past_len, D)
    y_s, ab_s, sb_s = _trunk(
        x_sample, (flat(cache_fox_k[0]), flat(cache_fox_v[0]), cache_fox_logf[0]),
        state_ssm[0].reshape(BS, D, SSD_STATE), state_conv[0],
        (flat(cache_sb_k[0]), flat(cache_sb_v[0])),
        cache_mem_k.reshape(depth, BS, M, D), cache_mem_v.reshape(depth, BS, M, D), W)

    def pack(ab, sb, B, L):
        k, v, logf, ssm, conv = ab
        k2, v2 = sb
        heads = lambda a, dh: a.reshape(1, B, L, D // dh, dh)
        return (heads(k, FOX_HEAD_DIM), heads(v, FOX_HEAD_DIM), logf[None],
                ssm.reshape(1, B, ssd_heads, SSD_HEAD_DIM, SSD_STATE), conv[None],
                heads(k2, SB_HEAD_DIM), heads(v2, SB_HEAD_DIM))

    mem_shape = (depth, BP, M, CA_HEADS, D // CA_HEADS)
    return ((y_p, y_s) + pack(ab_p, sb_p, BP, LP)
            + (mem_k_p.reshape(mem_shape), mem_v_p.reshape(mem_shape)) + pack(ab_s, sb_s, BS, LS))
```
